```python
import math
import jax, jax.numpy as jnp
from jax import lax
import numpy as np

D_MODEL = 4096
BATCH = 2
SEQ = 8192
DEPTH = 2

CHUNK = 64
Q_BLOCK = 128
HEAD_DIM = 128
N_MIX_HEADS = D_MODEL // HEAD_DIM
MIX_WIDTH = N_MIX_HEADS * HEAD_DIM
MLSTM_HEADS = N_MIX_HEADS // 4
DIFF_HEADS = (N_MIX_HEADS - MLSTM_HEADS) // 2
FOX_HEADS = N_MIX_HEADS - MLSTM_HEADS - DIFF_HEADS
DIFF_WIDTH = DIFF_HEADS * HEAD_DIM
MLSTM_WIDTH = MLSTM_HEADS * HEAD_DIM
FOX_WIDTH = FOX_HEADS * HEAD_DIM
DIFF_QK_DIM = HEAD_DIM // 2
CONV_WIDTH = 4
D_FF = 256 * ((8 * D_MODEL // 3 + 255) // 256)
NORM_EPS = 1e-6
IN_SPLIT_SIZES = (DIFF_WIDTH,) * 3 + (MLSTM_WIDTH,) * 4 + (MLSTM_HEADS,) * 2 + (FOX_WIDTH,) * 3 + (FOX_HEADS,)
N_IN = sum(IN_SPLIT_SIZES)
IN_SPLIT_POINTS = tuple(np.cumsum(IN_SPLIT_SIZES)[:-1].tolist())

kernel_name = 'hybrid_diffattn_mlstm_fox_macaron'


def rms_norm(x, g):
    xf = x.astype(jnp.float32)
    y = xf * lax.rsqrt(jnp.mean(xf * xf, axis=-1, keepdims=True) + NORM_EPS)
    return (y * g.astype(jnp.float32)).astype(x.dtype)


def swiglu(x, w_gate, w_up, w_down):
    return (jax.nn.silu(x @ w_gate) * (x @ w_up)) @ w_down


def causal_depthwise_conv(x, w, b):
    width = w.shape[0]
    y = lax.conv_general_dilated(
        x, w[:, None, :].astype(x.dtype), window_strides=(1,), padding=[(width - 1, 0)],
        dimension_numbers=('NWC', 'WIO', 'NWC'), feature_group_count=x.shape[-1])
    return y + b


def diff_attention(q, k, v, lam, subln, lam_init):
    bsz, seq, heads, _, dq = q.shape
    nb = seq // Q_BLOCK
    q = q * (dq ** -0.5)
    q_blocks = jnp.moveaxis(q.reshape(bsz, nb, Q_BLOCK, heads, 2, dq), 1, 0)
    key_chunk = jnp.arange(seq) // CHUNK

    def one_block(args):
        q_blk, blk = args
        q_chunk = (blk * Q_BLOCK + jnp.arange(Q_BLOCK)) // CHUNK
        visible = key_chunk[None, :] <= q_chunk[:, None]
        logits = jnp.einsum('bqhcd,bkhcd->bhcqk', q_blk, k).astype(jnp.float32)
        logits = jnp.where(visible, logits, -jnp.inf)
        probs = jax.nn.softmax(logits, axis=-1)
        weights = probs[:, :, 0] - lam * probs[:, :, 1]
        return jnp.einsum('bhqk,bkhd->bqhd', weights.astype(v.dtype), v)

    out = lax.map(one_block, (q_blocks, jnp.arange(nb)))
    out = jnp.moveaxis(out, 0, 1).reshape(bsz, seq, heads, -1)
    return rms_norm(out, subln) * (1.0 - lam_init)


def mlstm_chunkwise(q, k, v, i_pre, f_pre):
    bsz, seq, heads, d = q.shape
    nc = seq // CHUNK
    f32 = jnp.float32

    def to_chunks(a):
        return a.reshape(bsz, nc, CHUNK, heads, -1).transpose(0, 3, 1, 2, 4).astype(f32)

    q = to_chunks(q)
    k = to_chunks(k) * (d ** -0.5)
    v = to_chunks(v)
    ig = i_pre.astype(f32).reshape(bsz, nc, CHUNK, heads).transpose(0, 3, 1, 2)
    lf = jax.nn.log_sigmoid(f_pre.astype(f32)).reshape(bsz, nc, CHUNK, heads).transpose(0, 3, 1, 2)
    b = jnp.cumsum(lf, axis=-1)
    g = b[..., -1]

    a = g[..., None] - b + ig
    m_loc = jnp.max(a, axis=-1)
    w_loc = jnp.exp(a - m_loc[..., None])
    c_loc = jnp.einsum('bhcl,bhcld,bhcle->bhcde', w_loc, k, v)
    n_loc = jnp.einsum('bhcl,bhcld->bhcd', w_loc, k)

    def step(carry, xs):
        c_st, n_st, m_st = carry
        c_l, n_l, m_l, g_c = xs
        m_new = jnp.maximum(g_c + m_st, m_l)
        decay = jnp.exp(g_c + m_st - m_new)
        scale = jnp.exp(m_l - m_new)
        c_new = decay[..., None, None] * c_st + scale[..., None, None] * c_l
        n_new = decay[..., None] * n_st + scale[..., None] * n_l
        return (c_new, n_new, m_new), (c_st, n_st, m_st)

    init = (jnp.zeros((bsz, heads, d, d), f32), jnp.zeros((bsz, heads, d), f32),
            jnp.zeros((bsz, heads), f32))
    xs = (jnp.moveaxis(c_loc, 2, 0), jnp.moveaxis(n_loc, 2, 0),
          jnp.moveaxis(m_loc, 2, 0), jnp.moveaxis(g, 2, 0))
    _, (c_prev, n_prev, m_prev) = lax.scan(step, init, xs)
    c_prev = jnp.moveaxis(c_prev, 0, 2)
    n_prev = jnp.moveaxis(n_prev, 0, 2)
    m_prev = jnp.moveaxis(m_prev, 0, 2)

    causal = jnp.tril(jnp.ones((CHUNK, CHUNK), dtype=bool))
    log_d = b[..., :, None] - b[..., None, :] + ig[..., None, :]
    log_d = jnp.where(causal, log_d, -jnp.inf)
    log_inter = b + m_prev[..., None]
    m_t = jnp.maximum(log_inter, jnp.max(log_d, axis=-1))
    s_qk = jnp.einsum('bhcld,bhcsd->bhcls', q, k) * jnp.exp(log_d - m_t[..., None])
    w_inter = jnp.exp(log_inter - m_t)
    num = (w_inter[..., None] * jnp.einsum('bhcld,bhcde->bhcle', q, c_prev)
           + jnp.einsum('bhcls,bhcse->bhcle', s_qk, v))
    den = w_inter * jnp.einsum('bhcld,bhcd->bhcl', q, n_prev) + jnp.sum(s_qk, axis=-1)
    h = num / jnp.maximum(jnp.abs(den), jnp.exp(-m_t))[..., None]
    return h.transpose(0, 2, 3, 1, 4).reshape(bsz, seq, heads, d)


def forgetting_attention(q, k, v, log_f):
    bsz, seq, heads, d = q.shape
    nb = seq // Q_BLOCK
    q = q * (d ** -0.5)
    cum = jnp.cumsum(log_f, axis=1).transpose(0, 2, 1)
    q_blocks = jnp.moveaxis(q.reshape(bsz, nb, Q_BLOCK, heads, d), 1, 0)
    cum_blocks = jnp.moveaxis(cum.reshape(bsz, heads, nb, Q_BLOCK), 2, 0)
    key_pos = jnp.arange(seq)

    def one_block(args):
        q_blk, cum_blk, blk = args
        q_pos = blk * Q_BLOCK + jnp.arange(Q_BLOCK)
        visible = key_pos[None, :] <= q_pos[:, None]
        logits = (jnp.einsum('bqhd,bkhd->bhqk', q_blk, k).astype(jnp.float32)
                  + cum_blk[..., :, None] - cum[:, :, None, :])
        logits = jnp.where(visible, logits, -jnp.inf)
        probs = jax.nn.softmax(logits, axis=-1)
        return jnp.einsum('bhqk,bkhd->bqhd', probs.astype(v.dtype), v)

    out = lax.map(one_block, (q_blocks, cum_blocks, jnp.arange(nb)))
    return jnp.moveaxis(out, 0, 1).reshape(bsz, seq, heads, d)


def _normal(key, shape, scale):
    return scale * jax.random.normal(key, shape, jnp.float32)


def setup_inputs(seed: int = 0) -> dict:
    key = jax.random.key(seed)
    ks = jax.random.split(key, 32)
    d, f = D_MODEL, D_FF
    gain = lambda kk, shape: 1.0 + _normal(kk, shape, 0.02)
    return {
        'x': _normal(ks[0], (BATCH, SEQ, d), 1.0),
        'ffn1_norm': gain(ks[1], (DEPTH, d)),
        'ffn1_w_gate': _normal(ks[2], (DEPTH, d, f), d ** -0.5),
        'ffn1_w_up': _normal(ks[3], (DEPTH, d, f), d ** -0.5),
        'ffn1_w_down': _normal(ks[4], (DEPTH, f, d), f ** -0.5),
        'mix_norm': gain(ks[5], (DEPTH, d)),
        'w_in': _normal(ks[6], (DEPTH, d, N_IN), d ** -0.5),
        'diff_lambda_q1': _normal(ks[7], (DEPTH, DIFF_QK_DIM), 0.1),
        'diff_lambda_k1': _normal(ks[8], (DEPTH, DIFF_QK_DIM), 0.1),
        'diff_lambda_q2': _normal(ks[9], (DEPTH, DIFF_QK_DIM), 0.1),
        'diff_lambda_k2': _normal(ks[10], (DEPTH, DIFF_QK_DIM), 0.1),
        'diff_subln': gain(ks[11], (DEPTH, HEAD_DIM)),
        'mlstm_conv_w': _normal(ks[12], (DEPTH, CONV_WIDTH, 2 * MLSTM_WIDTH), CONV_WIDTH ** -0.5),
        'mlstm_conv_b': _normal(ks[13], (DEPTH, 2 * MLSTM_WIDTH), 0.02),
        'mlstm_b_i': _normal(ks[14], (DEPTH, MLSTM_HEADS), 0.1),
        'mlstm_b_f': jnp.linspace(3.0, 6.0, MLSTM_HEADS, dtype=jnp.float32)[None, :]
                     + _normal(ks[15], (DEPTH, MLSTM_HEADS), 0.1),
        'mlstm_norm': gain(ks[16], (DEPTH, MLSTM_WIDTH)),
        'fox_b_f': jnp.linspace(1.0, 5.0, FOX_HEADS, dtype=jnp.float32)[None, :]
                   + _normal(ks[17], (DEPTH, FOX_HEADS), 0.1),
        'w_out': _normal(ks[18], (DEPTH, MIX_WIDTH, d), MIX_WIDTH ** -0.5),
        'ffn2_norm': gain(ks[19], (DEPTH, d)),
        'ffn2_w_gate': _normal(ks[20], (DEPTH, d, f), d ** -0.5),
        'ffn2_w_up': _normal(ks[21], (DEPTH, d, f), d ** -0.5),
        'ffn2_w_down': _normal(ks[22], (DEPTH, f, d), f ** -0.5),
        'final_norm': gain(ks[23], (d,)),
    }


def reference(x, ffn1_norm, ffn1_w_gate, ffn1_w_up, ffn1_w_down, mix_norm, w_in,
              diff_lambda_q1, diff_lambda_k1, diff_lambda_q2, diff_lambda_k2, diff_subln,
              mlstm_conv_w, mlstm_conv_b, mlstm_b_i, mlstm_b_f, mlstm_norm, fox_b_f, w_out,
              ffn2_norm, ffn2_w_gate, ffn2_w_up, ffn2_w_down, final_norm):
    bsz, seq, _ = x.shape
    f32 = jnp.float32
    for l in range(DEPTH):
        x = x + 0.5 * swiglu(rms_norm(x, ffn1_norm[l]), ffn1_w_gate[l], ffn1_w_up[l], ffn1_w_down[l])

        h = rms_norm(x, mix_norm[l])
        proj = h @ w_in[l]
        (dq, dk, dv, mq, mk, mv, mo, mi, mf, fq, fk, fv, ff) = jnp.split(proj, IN_SPLIT_POINTS, axis=-1)

        lam_init = 0.8 - 0.6 * math.exp(-0.3 * l)
        lam = (jnp.exp(jnp.sum(diff_lambda_q1[l].astype(f32) * diff_lambda_k1[l].astype(f32)))
               - jnp.exp(jnp.sum(diff_lambda_q2[l].astype(f32) * diff_lambda_k2[l].astype(f32)))
               + lam_init)
        y_diff = diff_attention(
            dq.reshape(bsz, seq, DIFF_HEADS, 2, DIFF_QK_DIM),
            dk.reshape(bsz, seq, DIFF_HEADS, 2, DIFF_QK_DIM),
            dv.reshape(bsz, seq, DIFF_HEADS, HEAD_DIM),
            lam, diff_subln[l], lam_init).reshape(bsz, seq, DIFF_WIDTH)

        qk = jax.nn.silu(causal_depthwise_conv(jnp.concatenate([mq, mk], axis=-1), mlstm_conv_w[l], mlstm_conv_b[l]))
        mq_c, mk_c = jnp.split(qk, 2, axis=-1)
        h_tilde = mlstm_chunkwise(
            mq_c.reshape(bsz, seq, MLSTM_HEADS, HEAD_DIM),
            mk_c.reshape(bsz, seq, MLSTM_HEADS, HEAD_DIM),
            mv.reshape(bsz, seq, MLSTM_HEADS, HEAD_DIM),
            mi + mlstm_b_i[l], mf + mlstm_b_f[l])
        h_m = jax.nn.sigmoid(mo.astype(f32)).reshape(bsz, seq, MLSTM_HEADS, HEAD_DIM) * h_tilde
        y_mlstm = rms_norm(h_m, mlstm_norm[l].reshape(MLSTM_HEADS, HEAD_DIM)).astype(x.dtype)
        y_mlstm = y_mlstm.reshape(bsz, seq, MLSTM_WIDTH)

        log_f = jax.nn.log_sigmoid((ff + fox_b_f[l]).astype(f32))
        y_fox = forgetting_attention(
            fq.reshape(bsz, seq, FOX_HEADS, HEAD_DIM),
            fk.reshape(bsz, seq, FOX_HEADS, HEAD_DIM),
            fv.reshape(bsz, seq, FOX_HEADS, HEAD_DIM),
            log_f).reshape(bsz, seq, FOX_WIDTH)

        y = jnp.concatenate([y_diff.astype(x.dtype), y_mlstm, y_fox.astype(x.dtype)], axis=-1)
        x = x + y @ w_out[l]

        x = x + 0.5 * swiglu(rms_norm(x, ffn2_norm[l]), ffn2_w_gate[l], ffn2_w_up[l], ffn2_w_down[l])
    return rms_norm(x, final_norm)
```

```python
import functools
import math

import jax
import jax.numpy as jnp
from jax import lax
from jax.experimental import pallas as pl
from jax.experimental.pallas import tpu as pltpu

HEAD_DIM = 128
DIFF_QK_DIM = HEAD_DIM // 2
CONV_WIDTH = 4
NORM_EPS = 1e-6
LANES = 128
SUBLANES = 8
VMEM_CAP = 60 * 1024 * 1024
NEG_BIG = -1e30

BF16 = jnp.bfloat16
F32 = jnp.float32


def _vmem(nbytes):
    return int(min(VMEM_CAP, nbytes + (8 << 20)))


def _dot(a, b):
    return jnp.dot(a, b, preferred_element_type=F32)


def _dot_nt(a, b):
    return lax.dot_general(a, b, (((1,), (1,)), ((), ())), preferred_element_type=F32)


def _dot_tn(a, b):
    return lax.dot_general(a, b, (((0,), (0,)), ((), ())), preferred_element_type=F32)


def _rms_rows(xf, g):
    ms = jnp.mean(xf * xf, axis=-1, keepdims=True)
    return xf * lax.rsqrt(ms + NORM_EPS) * g


def _sigmoid(z):
    return 1.0 / (1.0 + jnp.exp(-z))


def _silu(z):
    return z * _sigmoid(z)


def _ffn_kernel(x_ref, g_ref, wg_ref, wu_ref, wd_ref, fg_ref, o_ref, h_ref, *, row_chunk, final):
    k = pl.program_id(1)
    tm = x_ref.shape[0]
    n_chunks = tm // row_chunk

    @pl.when(k == 0)
    def _():
        def body(r, c):
            rows = pl.ds(pl.multiple_of(r * row_chunk, row_chunk), row_chunk)
            xf = x_ref[rows, :]
            h_ref[rows, :] = _rms_rows(xf, g_ref[...]).astype(BF16)
            o_ref[rows, :] = xf
            return c
        lax.fori_loop(0, n_chunks, body, 0)

    h = h_ref[...]
    gate = _dot(h, wg_ref[...])
    up = _dot(h, wu_ref[...])
    act = (0.5 * _silu(gate) * up).astype(BF16)
    o_ref[...] += _dot(act, wd_ref[...])

    if final:
        @pl.when(k == pl.num_programs(1) - 1)
        def _():
            def body(r, c):
                rows = pl.ds(pl.multiple_of(r * row_chunk, row_chunk), row_chunk)
                o_ref[rows, :] = _rms_rows(o_ref[rows, :], fg_ref[...])
                return c
            lax.fori_loop(0, n_chunks, body, 0)


def _ffn(x, g, wg, wu, wd, fg, *, final):
    m, d = x.shape
    f = wg.shape[1]
    tm = min(512, m)
    tf = 256 if f % 256 == 0 else LANES
    row_chunk = min(64, tm)
    need = (2 * tm * d * 4) * 2 + tm * d * 2 + 2 * (3 * d * tf * 2) + 4 * tm * tf * 4
    return pl.pallas_call(
        functools.partial(_ffn_kernel, row_chunk=row_chunk, final=final),
        out_shape=jax.ShapeDtypeStruct((m, d), F32),
        grid=(m // tm, f // tf),
        in_specs=[
            pl.BlockSpec((tm, d), lambda i, k: (i, 0)),
            pl.BlockSpec((1, d), lambda i, k: (0, 0)),
            pl.BlockSpec((d, tf), lambda i, k: (0, k)),
            pl.BlockSpec((d, tf), lambda i, k: (0, k)),
            pl.BlockSpec((tf, d), lambda i, k: (k, 0)),
            pl.BlockSpec((1, d), lambda i, k: (0, 0)),
        ],
        out_specs=pl.BlockSpec((tm, d), lambda i, k: (i, 0)),
        scratch_shapes=[pltpu.VMEM((tm, d), BF16)],
        compiler_params=pltpu.CompilerParams(
            dimension_semantics=("parallel", "arbitrary"),
            vmem_limit_bytes=_vmem(need)),
        name="ffn_half_step",
    )(x, g, wg, wu, wd, fg)


def _proj_kernel(x_ref, g_ref, w_ref, o_ref, h_ref, *, row_chunk):
    j = pl.program_id(1)
    tm = x_ref.shape[0]

    @pl.when(j == 0)
    def _():
        def body(r, c):
            rows = pl.ds(pl.multiple_of(r * row_chunk, row_chunk), row_chunk)
            h_ref[rows, :] = _rms_rows(x_ref[rows, :], g_ref[...]).astype(BF16)
            return c
        lax.fori_loop(0, tm // row_chunk, body, 0)

    o_ref[...] = _dot(h_ref[...], w_ref[...]).astype(o_ref.dtype)


def _pick_tn(n, cap):
    best = LANES
    for t in range(LANES, cap + 1, LANES):
        if n % t == 0:
            best = t
    return best


def _proj(x, g, w, out_dtype, name):
    m, d = x.shape
    n = w.shape[1]
    tm = min(512, m)
    tn = _pick_tn(n, 1024)
    row_chunk = min(64, tm)
    osz = jnp.dtype(out_dtype).itemsize
    need = 2 * tm * d * 4 + tm * d * 2 + 2 * d * tn * 2 + 2 * tm * tn * osz + tm * tn * 4
    return pl.pallas_call(
        functools.partial(_proj_kernel, row_chunk=row_chunk),
        out_shape=jax.ShapeDtypeStruct((m, n), out_dtype),
        grid=(m // tm, n // tn),
        in_specs=[
            pl.BlockSpec((tm, d), lambda i, j: (i, 0)),
            pl.BlockSpec((1, d), lambda i, j: (0, 0)),
            pl.BlockSpec((d, tn), lambda i, j: (0, j)),
        ],
        out_specs=pl.BlockSpec((tm, tn), lambda i, j: (i, j)),
        scratch_shapes=[pltpu.VMEM((tm, d), BF16)],
        compiler_params=pltpu.CompilerParams(
            dimension_semantics=("parallel", "arbitrary"),
            vmem_limit_bytes=_vmem(need)),
        name=name,
    )(x, g, w)


def _split3(v):
    hi = v.astype(BF16)
    r1 = v - hi.astype(F32)
    mid = r1.astype(BF16)
    lo = (r1 - mid.astype(F32)).astype(BF16)
    return hi, mid, lo


def _gates_kernel(p_ref, bias_ref, ga_ref, gt_ref, carry_ref, *, hm, hf):
    r = pl.program_id(1)
    rows = p_ref.shape[0]

    @pl.when(r == 0)
    def _():
        carry_ref[...] = jnp.zeros_like(carry_ref)

    z = p_ref[...] + bias_ref[...]
    ls = jnp.minimum(z, 0.0) - jnp.log(1.0 + jnp.exp(-jnp.abs(z)))
    ri = lax.broadcasted_iota(jnp.int32, (rows, rows), 0)
    ci = lax.broadcasted_iota(jnp.int32, (rows, rows), 1)
    tri = jnp.where(ci <= ri, 1.0, 0.0).astype(BF16)
    hi, mid, lo = _split3(ls)
    cs = _dot(tri, hi) + _dot(tri, mid) + _dot(tri, lo)
    lane = lax.broadcasted_iota(jnp.int32, (rows, LANES), 1)
    run = cs + carry_ref[...]
    ga = jnp.where(lane < hm, z,
                   jnp.where(lane < 2 * hm, cs,
                             jnp.where(lane < 2 * hm + hf, run, 0.0)))
    ga_ref[...] = ga
    gt_ref[...] = ga.T
    carry_ref[...] = run[rows - 1:rows, :]


def _gates(p32, bias, col_block, bsz, seq, hm, hf, blk):
    return pl.pallas_call(
        functools.partial(_gates_kernel, hm=hm, hf=hf),
        out_shape=(jax.ShapeDtypeStruct((bsz, seq, LANES), F32),
                   jax.ShapeDtypeStruct((bsz, LANES, seq), F32)),
        grid=(bsz, seq // blk),
        in_specs=[
            pl.BlockSpec((None, blk, LANES), lambda b, r: (b, r, col_block)),
            pl.BlockSpec((1, LANES), lambda b, r: (0, 0)),
        ],
        out_specs=(pl.BlockSpec((None, blk, LANES), lambda b, r: (b, r, 0)),
                   pl.BlockSpec((None, LANES, blk), lambda b, r: (b, 0, r))),
        scratch_shapes=[pltpu.VMEM((1, LANES), F32)],
        compiler_params=pltpu.CompilerParams(
            dimension_semantics=("parallel", "arbitrary")),
        name="gates",
    )(p32, bias)


def _online_softmax_step(s, vb, m, l, acc):
    m_new = jnp.maximum(m, jnp.max(s, axis=1, keepdims=True))
    alpha = jnp.exp(m - m_new)
    p = jnp.exp(s - m_new)
    l = alpha * l + jnp.sum(p, axis=1, keepdims=True)
    acc = alpha * acc + _dot(p.astype(BF16), vb)
    return m_new, l, acc


def _diff_kernel(q_ref, k_ref, v_ref, lam_ref, sub_ref, o_ref, *, tq, chunk, lam_init):
    i = pl.program_id(2)
    lane = lax.broadcasted_iota(jnp.int32, (tq, HEAD_DIM), 1)
    qf = q_ref[...].astype(F32) * (DIFF_QK_DIM ** -0.5)
    q0 = jnp.where(lane < DIFF_QK_DIM, qf, 0.0)
    q1 = jnp.where(lane >= DIFF_QK_DIM, qf, 0.0)
    qq = jnp.concatenate([q0, q1], axis=0).astype(BF16)

    def kv(j):
        rows = pl.ds(pl.multiple_of(j * tq, tq), tq)
        return k_ref[rows, :], v_ref[rows, :]

    def body(j, carry):
        kb, vb = kv(j)
        return _online_softmax_step(_dot_nt(qq, kb), vb, *carry)

    init = (jnp.full((2 * tq, 1), NEG_BIG, F32), jnp.zeros((2 * tq, 1), F32),
            jnp.zeros((2 * tq, HEAD_DIM), F32))
    carry = lax.fori_loop(0, i, body, init)

    kb, vb = kv(i)
    s = _dot_nt(qq, kb)
    row = lax.broadcasted_iota(jnp.int32, (2 * tq, tq), 0)
    col = lax.broadcasted_iota(jnp.int32, (2 * tq, tq), 1)
    qrow = jnp.where(row >= tq, row - tq, row)
    vis = (col // chunk) <= (qrow // chunk)
    s = jnp.where(vis, s, NEG_BIG)
    m, l, acc = _online_softmax_step(s, vb, *carry)

    o = acc / l
    lp = lam_ref[...]
    lam = (jnp.exp(jnp.sum(lp[0:1, :] * lp[1:2, :], axis=1, keepdims=True))
           - jnp.exp(jnp.sum(lp[2:3, :] * lp[3:4, :], axis=1, keepdims=True)) + lam_init)
    out = o[:tq, :] - lam * o[tq:, :]
    o_ref[...] = (_rms_rows(out, sub_ref[...]) * (1.0 - lam_init)).astype(o_ref.dtype)


def _diff_attention(pb, lam_p, subln, *, bsz, seq, heads, qcol, kcol, vcol, chunk, lam_init, tq):
    width = heads * HEAD_DIM
    need = 2 * (2 * seq * HEAD_DIM * 2) + 16 * tq * tq * 4
    return pl.pallas_call(
        functools.partial(_diff_kernel, tq=tq, chunk=chunk, lam_init=lam_init),
        out_shape=jax.ShapeDtypeStruct((bsz, seq, width), BF16),
        grid=(bsz, heads, seq // tq),
        in_specs=[
            pl.BlockSpec((None, tq, HEAD_DIM), lambda b, h, i: (b, i, qcol + h)),
            pl.BlockSpec((None, seq, HEAD_DIM), lambda b, h, i: (b, 0, kcol + h)),
            pl.BlockSpec((None, seq, HEAD_DIM), lambda b, h, i: (b, 0, vcol + h)),
            pl.BlockSpec((4, DIFF_QK_DIM), lambda b, h, i: (0, 0)),
            pl.BlockSpec((1, HEAD_DIM), lambda b, h, i: (0, 0)),
        ],
        out_specs=pl.BlockSpec((None, tq, HEAD_DIM), lambda b, h, i: (b, i, h)),
        compiler_params=pltpu.CompilerParams(
            dimension_semantics=("parallel", "parallel", "arbitrary"),
            vmem_limit_bytes=_vmem(need)),
        name="diff_attention",
    )(pb, pb, pb, lam_p, subln)


def _fox_kernel(q_ref, k_ref, v_ref, ga_ref, gt_ref, o_ref, *, tq, lane0):
    h = pl.program_id(1)
    i = pl.program_id(2)
    scale = HEAD_DIM ** -0.5
    q = (q_ref[...].astype(F32) * scale).astype(BF16)
    lane = lax.broadcasted_iota(jnp.int32, (tq, LANES), 1)
    cq = jnp.sum(jnp.where(lane == lane0 + h, ga_ref[...], 0.0), axis=1, keepdims=True)

    def blk(j):
        start = pl.multiple_of(j * tq, tq)
        rows = pl.ds(start, tq)
        ck = gt_ref[pl.ds(lane0 + h, 1), pl.ds(start, tq)]
        return _dot_nt(q, k_ref[rows, :]) + (cq - ck), v_ref[rows, :]

    def body(j, carry):
        s, vb = blk(j)
        return _online_softmax_step(s, vb, *carry)

    init = (jnp.full((tq, 1), NEG_BIG, F32), jnp.zeros((tq, 1), F32),
            jnp.zeros((tq, HEAD_DIM), F32))
    carry = lax.fori_loop(0, i, body, init)

    s, vb = blk(i)
    row = lax.broadcasted_iota(jnp.int32, (tq, tq), 0)
    col = lax.broadcasted_iota(jnp.int32, (tq, tq), 1)
    s = jnp.where(col <= row, s, NEG_BIG)
    m, l, acc = _online_softmax_step(s, vb, *carry)
    o_ref[...] = (acc / l).astype(o_ref.dtype)


def _fox_attention(pb, ga, gt, *, bsz, seq, heads, qcol, kcol, vcol, lane0, tq):
    width = heads * HEAD_DIM
    need = 2 * (2 * seq * HEAD_DIM * 2) + 2 * LANES * seq * 4 + 16 * tq * tq * 4
    return pl.pallas_call(
        functools.partial(_fox_kernel, tq=tq, lane0=lane0),
        out_shape=jax.ShapeDtypeStruct((bsz, seq, width), BF16),
        grid=(bsz, heads, seq // tq),
        in_specs=[
            pl.BlockSpec((None, tq, HEAD_DIM), lambda b, h, i: (b, i, qcol + h)),
            pl.BlockSpec((None, seq, HEAD_DIM), lambda b, h, i: (b, 0, kcol + h)),
            pl.BlockSpec((None, seq, HEAD_DIM), lambda b, h, i: (b, 0, vcol + h)),
            pl.BlockSpec((None, tq, LANES), lambda b, h, i: (b, i, 0)),
            pl.BlockSpec((None, LANES, seq), lambda b, h, i: (b, 0, 0)),
        ],
        out_specs=pl.BlockSpec((None, tq, HEAD_DIM), lambda b, h, i: (b, i, h)),
        compiler_params=pltpu.CompilerParams(
            dimension_semantics=("parallel", "parallel", "arbitrary"),
            vmem_limit_bytes=_vmem(need)),
        name="fox_attention",
    )(pb, pb, pb, ga, gt)


def _mlstm_kernel(q_ref, k_ref, o_gate_ref, v_ref, ga_ref, gt_ref, cwq_ref, cwk_ref, cbq_ref, cbk_ref,
                  g_ref, y_ref, qbuf, kbuf, c_ref, n_ref, m_ref, *, hm):
    h = pl.program_id(1)
    r = pl.program_id(2)
    ln = q_ref.shape[0]
    halo = SUBLANES

    @pl.when(r == 0)
    def _():
        qbuf[0:halo, :] = jnp.zeros((halo, HEAD_DIM), F32)
        kbuf[0:halo, :] = jnp.zeros((halo, HEAD_DIM), F32)
        c_ref[...] = jnp.zeros_like(c_ref)
        n_ref[...] = jnp.zeros_like(n_ref)
        m_ref[...] = jnp.zeros_like(m_ref)

    @pl.when(r > 0)
    def _():
        qbuf[0:halo, :] = qbuf[ln:ln + halo, :]
        kbuf[0:halo, :] = kbuf[ln:ln + halo, :]

    qbuf[halo:halo + ln, :] = q_ref[...]
    kbuf[halo:halo + ln, :] = k_ref[...]

    def conv_silu(buf, w_ref, b_ref):
        acc = b_ref[...] + w_ref[CONV_WIDTH - 1:CONV_WIDTH, :] * buf[halo:halo + ln, :]
        for t in range(1, CONV_WIDTH):
            acc = acc + w_ref[CONV_WIDTH - 1 - t:CONV_WIDTH - t, :] * buf[halo - t:halo - t + ln, :]
        return _silu(acc)

    q = conv_silu(qbuf, cwq_ref, cbq_ref)
    k = conv_silu(kbuf, cwk_ref, cbk_ref) * (HEAD_DIM ** -0.5)
    v = v_ref[...]

    lane = lax.broadcasted_iota(jnp.int32, (ln, LANES), 1)
    ga = ga_ref[...]
    ig_c = jnp.sum(jnp.where(lane == h, ga, 0.0), axis=1, keepdims=True)
    b_c = jnp.sum(jnp.where(lane == hm + h, ga, 0.0), axis=1, keepdims=True)
    ig_r = gt_ref[pl.ds(h, 1), :]
    b_r = gt_ref[pl.ds(hm + h, 1), :]
    g = b_r[:, ln - 1:ln]

    c_prev = c_ref[...]
    n_prev = n_ref[...]
    m_prev = m_ref[...]

    a = g - b_c + ig_c
    m_loc = jnp.max(a, axis=0, keepdims=True)
    kw = (k * jnp.exp(a - m_loc))
    c_loc = _dot_tn(kw.astype(BF16), v)
    n_loc = jnp.sum(kw, axis=0, keepdims=True)
    m_new = jnp.maximum(g + m_prev, m_loc)
    decay = jnp.exp(g + m_prev - m_new)
    scl = jnp.exp(m_loc - m_new)
    c_ref[...] = decay * c_prev + scl * c_loc
    n_ref[...] = decay * n_prev + scl * n_loc
    m_ref[...] = m_new

    row = lax.broadcasted_iota(jnp.int32, (ln, ln), 0)
    col = lax.broadcasted_iota(jnp.int32, (ln, ln), 1)
    log_d = jnp.where(col <= row, b_c - b_r + ig_r, NEG_BIG)
    log_inter = b_c + m_prev
    m_t = jnp.maximum(log_inter, jnp.max(log_d, axis=1, keepdims=True))
    qb = q.astype(BF16)
    s_qk = _dot_nt(qb, k.astype(BF16)) * jnp.exp(log_d - m_t)
    w_inter = jnp.exp(log_inter - m_t)
    num = w_inter * _dot(qb, c_prev.astype(BF16)) + _dot(s_qk.astype(BF16), v)
    den = w_inter * jnp.sum(q * n_prev, axis=1, keepdims=True) + jnp.sum(s_qk, axis=1, keepdims=True)
    hh = num / jnp.maximum(jnp.abs(den), jnp.exp(-m_t))
    hmix = _sigmoid(o_gate_ref[...]) * hh
    y_ref[...] = _rms_rows(hmix, g_ref[...]).astype(y_ref.dtype)


def _mlstm(p32, pb, ga, gt, conv_w, conv_b, gain, *, bsz, seq, hm, qcol, kcol, ocol, vcol, ln):
    width = hm * HEAD_DIM
    blk = lambda c: pl.BlockSpec((None, ln, HEAD_DIM), lambda b, h, r: (b, r, c + h))
    return pl.pallas_call(
        functools.partial(_mlstm_kernel, hm=hm),
        out_shape=jax.ShapeDtypeStruct((bsz, seq, width), BF16),
        grid=(bsz, hm, seq // ln),
        in_specs=[
            blk(qcol), blk(kcol), blk(ocol), blk(vcol),
            pl.BlockSpec((None, ln, LANES), lambda b, h, r: (b, r, 0)),
            pl.BlockSpec((None, LANES, ln), lambda b, h, r: (b, 0, r)),
            pl.BlockSpec((CONV_WIDTH, HEAD_DIM), lambda b, h, r: (0, h)),
            pl.BlockSpec((CONV_WIDTH, HEAD_DIM), lambda b, h, r: (0, hm + h)),
            pl.BlockSpec((1, HEAD_DIM), lambda b, h, r: (0, h)),
            pl.BlockSpec((1, HEAD_DIM), lambda b, h, r: (0, hm + h)),
            pl.BlockSpec((1, HEAD_DIM), lambda b, h, r: (0, h)),
        ],
        out_specs=pl.BlockSpec((None, ln, HEAD_DIM), lambda b, h, r: (b, r, h)),
        scratch_shapes=[
            pltpu.VMEM((ln + 2 * SUBLANES, HEAD_DIM), F32),
            pltpu.VMEM((ln + 2 * SUBLANES, HEAD_DIM), F32),
            pltpu.VMEM((HEAD_DIM, HEAD_DIM), F32),
            pltpu.VMEM((1, HEAD_DIM), F32),
            pltpu.VMEM((1, 1), F32),
        ],
        compiler_params=pltpu.CompilerParams(
            dimension_semantics=("parallel", "parallel", "arbitrary")),
        name="mlstm",
    )(p32, p32, p32, pb, ga, gt, conv_w, conv_w, conv_b, conv_b, gain)


def _oproj_kernel(x_ref, yd_ref, ym_ref, yf_ref, wd_ref, wm_ref, wf_ref, o_ref):
    acc = _dot(yd_ref[...], wd_ref[...])
    acc += _dot(ym_ref[...], wm_ref[...])
    acc += _dot(yf_ref[...], wf_ref[...])
    o_ref[...] = x_ref[...] + acc


def _oproj(x, yd, ym, yf, wd, wm, wf):
    m, d = x.shape
    tm = min(1024, m)
    tn = _pick_tn(d, 512)
    kd, km, kf = yd.shape[1], ym.shape[1], yf.shape[1]
    kk = kd + km + kf
    need = 2 * tm * kk * 2 + 2 * kk * tn * 2 + 5 * tm * tn * 4
    return pl.pallas_call(
        _oproj_kernel,
        out_shape=jax.ShapeDtypeStruct((m, d), F32),
        grid=(m // tm, d // tn),
        in_specs=[
            pl.BlockSpec((tm, tn), lambda i, j: (i, j)),
            pl.BlockSpec((tm, kd), lambda i, j: (i, 0)),
            pl.BlockSpec((tm, km), lambda i, j: (i, 0)),
            pl.BlockSpec((tm, kf), lambda i, j: (i, 0)),
            pl.BlockSpec((kd, tn), lambda i, j: (0, j)),
            pl.BlockSpec((km, tn), lambda i, j: (0, j)),
            pl.BlockSpec((kf, tn), lambda i, j: (0, j)),
        ],
        out_specs=pl.BlockSpec((tm, tn), lambda i, j: (i, j)),
        compiler_params=pltpu.CompilerParams(
            dimension_semantics=("parallel", "arbitrary"),
            vmem_limit_bytes=_vmem(need)),
        name="out_proj_residual",
    )(x, yd, ym, yf, wd, wm, wf)


def kernel(x, ffn1_norm, ffn1_w_gate, ffn1_w_up, ffn1_w_down, mix_norm, w_in, diff_lambda_q1, diff_lambda_k1, diff_lambda_q2, diff_lambda_k2, diff_subln, mlstm_conv_w, mlstm_conv_b, mlstm_b_i, mlstm_b_f, mlstm_norm, fox_b_f, w_out, ffn2_norm, ffn2_w_gate, ffn2_w_up, ffn2_w_down, final_norm):
    bsz, seq, d = x.shape
    depth = ffn1_norm.shape[0]
    m = bsz * seq
    n_heads = d // HEAD_DIM
    hm = n_heads // 4
    hd = (n_heads - hm) // 2
    hf = n_heads - hm - hd
    dw, mw, fw = hd * HEAD_DIM, hm * HEAD_DIM, hf * HEAD_DIM
    assert 2 * hm + hf <= LANES
    chunk = 64
    tq = min(256, seq)
    assert seq % tq == 0 and tq % chunk == 0

    o_dq, o_dk, o_dv = 0, dw, 2 * dw
    o_mq = 3 * dw
    o_mk, o_mv, o_mo = o_mq + mw, o_mq + 2 * mw, o_mq + 3 * mw
    o_mi = o_mq + 4 * mw
    o_mf = o_mi + hm
    o_fq = o_mf + hm
    o_fk, o_fv = o_fq + fw, o_fq + 2 * fw
    o_ff = o_fq + 3 * fw
    n_in = o_ff + hf
    assert w_in.shape[2] == n_in

    xf = x.reshape(m, d)
    row = lambda v: v.reshape(1, -1).astype(F32)
    final_g = row(final_norm)

    for l in range(depth):
        lam_init = 0.8 - 0.6 * math.exp(-0.3 * l)
        last = l == depth - 1

        xf = _ffn(xf, row(ffn1_norm[l]), ffn1_w_gate[l].astype(BF16), ffn1_w_up[l].astype(BF16),
                  ffn1_w_down[l].astype(BF16), final_g, final=False)

        wl = w_in[l]
        w_b = jnp.concatenate([wl[:, o_dq:o_mq], wl[:, o_fq:o_ff], wl[:, o_mv:o_mo]], axis=1).astype(BF16)
        gpad = jnp.zeros((d, LANES - 2 * hm - hf), wl.dtype)
        w_f = jnp.concatenate([wl[:, o_mq:o_mv], wl[:, o_mo:o_mi], wl[:, o_mi:o_fq], wl[:, o_ff:n_in], gpad],
                              axis=1).astype(BF16)
        g_mix = row(mix_norm[l])
        pb = _proj(xf, g_mix, w_b, BF16, "mix_proj_bf16").reshape(bsz, seq, -1)
        p32 = _proj(xf, g_mix, w_f, F32, "mix_proj_f32").reshape(bsz, seq, -1)

        gate_bias = jnp.concatenate([mlstm_b_i[l], mlstm_b_f[l], fox_b_f[l],
                                     jnp.zeros((LANES - 2 * hm - hf,), F32)]).reshape(1, LANES).astype(F32)
        ga, gt = _gates(p32, gate_bias, 3 * hm, bsz, seq, hm, hf, tq)

        lam_p = jnp.stack([diff_lambda_q1[l], diff_lambda_k1[l], diff_lambda_q2[l], diff_lambda_k2[l]]).astype(F32)
        y_diff = _diff_attention(pb, lam_p, row(diff_subln[l]), bsz=bsz, seq=seq, heads=hd,
                                 qcol=0, kcol=hd, vcol=2 * hd, chunk=chunk, lam_init=lam_init, tq=tq)
        y_fox = _fox_attention(pb, ga, gt, bsz=bsz, seq=seq, heads=hf,
                               qcol=3 * hd, kcol=3 * hd + hf, vcol=3 * hd + 2 * hf, lane0=2 * hm, tq=tq)
        y_ml = _mlstm(p32, pb, ga, gt, mlstm_conv_w[l].astype(F32), row(mlstm_conv_b[l]), row(mlstm_norm[l]),
                      bsz=bsz, seq=seq, hm=hm, qcol=0, kcol=hm, ocol=2 * hm, vcol=3 * hd + 3 * hf, ln=tq)

        wo = w_out[l]
        xf = _oproj(xf, y_diff.reshape(m, dw), y_ml.reshape(m, mw), y_fox.reshape(m, fw),
                    wo[:dw].astype(BF16), wo[dw:dw + mw].astype(BF16), wo[dw + mw:].astype(BF16))

        xf = _ffn(xf, row(ffn2_norm[l]), ffn2_w_gate[l].astype(BF16), ffn2_w_up[l].astype(BF16),
                  ffn2_w_down[l].astype(BF16), final_g, final=last)

    return xf.reshape(bsz, seq, d)
```

```python
import functools
import math

import jax
import jax.numpy as jnp
from jax import lax
from jax.experimental import pallas as pl
from jax.experimental.pallas import tpu as pltpu

HEAD_DIM = 128
DIFF_QK_DIM = HEAD_DIM // 2
CONV_WIDTH = 4
NORM_EPS = 1e-6
LANES = 128
SUBLANES = 8
VMEM_CAP = 60 * 1024 * 1024
NEG_BIG = -1e30

BF16 = jnp.bfloat16
F32 = jnp.float32


def _vmem(nbytes):
    return int(min(VMEM_CAP, nbytes + (8 << 20)))


def _dot(a, b):
    return jnp.dot(a, b, preferred_element_type=F32)


def _dot_nt(a, b):
    return lax.dot_general(a, b, (((1,), (1,)), ((), ())), preferred_element_type=F32)


def _dot_tn(a, b):
    return lax.dot_general(a, b, (((0,), (0,)), ((), ())), preferred_element_type=F32)


def _rms_rows(xf, g):
    ms = jnp.mean(xf * xf, axis=-1, keepdims=True)
    return xf * lax.rsqrt(ms + NORM_EPS) * g


def _sigmoid(z):
    return 1.0 / (1.0 + jnp.exp(-z))


def _silu(z):
    return z * _sigmoid(z)


def _cast_kernel(w_ref, o_ref):
    o_ref[...] = w_ref[...].astype(o_ref.dtype)


def _cast_bf16(w, layer, row0=0, rows=None):
    _, r_all, c = w.shape
    rows = r_all if rows is None else rows
    tr = 256
    while rows % tr or row0 % tr:
        tr //= 2
    assert tr >= 16
    return pl.pallas_call(
        _cast_kernel,
        out_shape=jax.ShapeDtypeStruct((rows, c), BF16),
        grid=(rows // tr,),
        in_specs=[pl.BlockSpec((None, tr, c), lambda r: (layer, row0 // tr + r, 0))],
        out_specs=pl.BlockSpec((tr, c), lambda r: (r, 0)),
        compiler_params=pltpu.CompilerParams(
            dimension_semantics=("parallel",),
            vmem_limit_bytes=_vmem(2 * tr * c * 6)),
        name="weight_cast",
    )(w)


def _ffn_kernel(x_ref, g_ref, wg_ref, wu_ref, wd_ref, fg_ref, o_ref, h_ref, *, row_chunk, final):
    k = pl.program_id(1)
    tm = x_ref.shape[0]
    n_chunks = tm // row_chunk

    @pl.when(k == 0)
    def _():
        def body(r, c):
            rows = pl.ds(pl.multiple_of(r * row_chunk, row_chunk), row_chunk)
            xf = x_ref[rows, :]
            h_ref[rows, :] = _rms_rows(xf, g_ref[...]).astype(BF16)
            o_ref[rows, :] = xf
            return c
        lax.fori_loop(0, n_chunks, body, 0)

    h = h_ref[...]
    gate = _dot(h, wg_ref[...])
    up = _dot(h, wu_ref[...])
    act = (0.5 * _silu(gate) * up).astype(BF16)
    o_ref[...] += _dot(act, wd_ref[...])

    if final:
        @pl.when(k == pl.num_programs(1) - 1)
        def _():
            def body(r, c):
                rows = pl.ds(pl.multiple_of(r * row_chunk, row_chunk), row_chunk)
                o_ref[rows, :] = _rms_rows(o_ref[rows, :], fg_ref[...])
                return c
            lax.fori_loop(0, n_chunks, body, 0)


def _ffn(x, g, wg, wu, wd, fg, *, final):
    m, d = x.shape
    f = wg.shape[1]
    tm = min(512, m)
    tf = 256 if f % 256 == 0 else LANES
    row_chunk = min(64, tm)
    need = (2 * tm * d * 4) * 2 + tm * d * 2 + 2 * (3 * d * tf * 2) + 4 * tm * tf * 4
    return pl.pallas_call(
        functools.partial(_ffn_kernel, row_chunk=row_chunk, final=final),
        out_shape=jax.ShapeDtypeStruct((m, d), F32),
        grid=(m // tm, f // tf),
        in_specs=[
            pl.BlockSpec((tm, d), lambda i, k: (i, 0)),
            pl.BlockSpec((1, d), lambda i, k: (0, 0)),
            pl.BlockSpec((d, tf), lambda i, k: (0, k)),
            pl.BlockSpec((d, tf), lambda i, k: (0, k)),
            pl.BlockSpec((tf, d), lambda i, k: (k, 0)),
            pl.BlockSpec((1, d), lambda i, k: (0, 0)),
        ],
        out_specs=pl.BlockSpec((tm, d), lambda i, k: (i, 0)),
        scratch_shapes=[pltpu.VMEM((tm, d), BF16)],
        compiler_params=pltpu.CompilerParams(
            dimension_semantics=("parallel", "arbitrary"),
            vmem_limit_bytes=_vmem(need)),
        name="ffn_half_step",
    )(x, g, wg, wu, wd, fg)


def _proj_kernel(x_ref, g_ref, w_ref, o_ref, h_ref, *, row_chunk, transpose_out):
    j = pl.program_id(1)
    tm = x_ref.shape[0]

    @pl.when(j == 0)
    def _():
        def body(r, c):
            rows = pl.ds(pl.multiple_of(r * row_chunk, row_chunk), row_chunk)
            h_ref[rows, :] = _rms_rows(x_ref[rows, :], g_ref[...]).astype(BF16)
            return c
        lax.fori_loop(0, tm // row_chunk, body, 0)

    res = _dot(h_ref[...], w_ref[...])
    o_ref[...] = (res.T if transpose_out else res).astype(o_ref.dtype)


def _pick_tn(n, cap):
    best = LANES
    for t in range(LANES, cap + 1, LANES):
        if n % t == 0:
            best = t
    return best


def _proj(x, g, w, out_dtype, name, transpose_out=False):
    m, d = x.shape
    n = w.shape[1]
    tm = min(512, m)
    tn = _pick_tn(n, 512 if transpose_out else 1024)
    row_chunk = min(64, tm)
    osz = jnp.dtype(out_dtype).itemsize
    need = 2 * tm * d * 4 + tm * d * 2 + 2 * d * tn * 2 + 2 * tm * tn * osz + 2 * tm * tn * 4
    if transpose_out:
        out_shape, out_spec = (n, m), pl.BlockSpec((tn, tm), lambda i, j: (j, i))
    else:
        out_shape, out_spec = (m, n), pl.BlockSpec((tm, tn), lambda i, j: (i, j))
    return pl.pallas_call(
        functools.partial(_proj_kernel, row_chunk=row_chunk, transpose_out=transpose_out),
        out_shape=jax.ShapeDtypeStruct(out_shape, out_dtype),
        grid=(m // tm, n // tn),
        in_specs=[
            pl.BlockSpec((tm, d), lambda i, j: (i, 0)),
            pl.BlockSpec((1, d), lambda i, j: (0, 0)),
            pl.BlockSpec((d, tn), lambda i, j: (0, j)),
        ],
        out_specs=out_spec,
        scratch_shapes=[pltpu.VMEM((tm, d), BF16)],
        compiler_params=pltpu.CompilerParams(
            dimension_semantics=("parallel", "arbitrary"),
            vmem_limit_bytes=_vmem(need)),
        name=name,
    )(x, g, w)


def _win_prep_kernel(w_ref, tail_ref, wb_ref, wv_ref, wf_ref, *, dw, mw, fw, sh, hf):
    tr = w_ref.shape[0]
    a_mq = 3 * dw
    a_mv, a_mo = a_mq + 2 * mw, a_mq + 3 * mw
    a_mi = a_mq + 4 * mw
    lane = lax.broadcasted_iota(jnp.int32, (tr, LANES), 1)

    wb_ref[:, 0:2 * dw] = w_ref[:, 0:2 * dw].astype(BF16)
    wb_ref[:, 2 * dw + 2 * fw:] = w_ref[:, a_mv:a_mo].astype(BF16)
    wv_ref[:, 0:dw] = w_ref[:, 2 * dw:3 * dw].astype(BF16)
    wf_ref[:, 0:2 * mw] = w_ref[:, a_mq:a_mv].astype(BF16)
    wf_ref[:, 2 * mw:3 * mw] = w_ref[:, a_mo:a_mi].astype(BF16)

    tail = tail_ref[...]
    gates = jnp.where(lane < sh, w_ref[:, a_mi:a_mi + LANES], jnp.where(lane < sh + hf, tail, 0.0))
    wf_ref[:, 3 * mw:] = gates.astype(BF16)

    n_blk = 3 * fw // LANES
    n_qk = 2 * fw // LANES

    def rolled(c):
        src = tail if c == n_blk else w_ref[:, a_mi + c * LANES:a_mi + (c + 1) * LANES]
        return pltpu.roll(src, LANES - sh, axis=1)

    prev = rolled(0)
    for c in range(n_blk):
        nxt = rolled(c + 1)
        blk = jnp.where(lane < LANES - sh, prev, nxt).astype(BF16)
        if c < n_qk:
            wb_ref[:, 2 * dw + c * LANES:2 * dw + (c + 1) * LANES] = blk
        else:
            wv_ref[:, dw + (c - n_qk) * LANES:dw + (c - n_qk + 1) * LANES] = blk
        prev = nxt


def _win_prep(w_in, tail, layer, *, dw, mw, fw, sh, hf):
    _, d, n_in = w_in.shape
    tr = min(LANES, d)
    nb, nv, nf = 2 * dw + 2 * fw + mw, dw + fw, 3 * mw + LANES
    need = 2 * tr * (n_in + LANES) * 4 + 2 * tr * (nb + nv + nf) * 2
    return pl.pallas_call(
        functools.partial(_win_prep_kernel, dw=dw, mw=mw, fw=fw, sh=sh, hf=hf),
        out_shape=(jax.ShapeDtypeStruct((d, nb), BF16), jax.ShapeDtypeStruct((d, nv), BF16),
                   jax.ShapeDtypeStruct((d, nf), BF16)),
        grid=(d // tr,),
        in_specs=[
            pl.BlockSpec((None, tr, n_in), lambda r: (layer, r, 0)),
            pl.BlockSpec((tr, LANES), lambda r: (r, 0)),
        ],
        out_specs=(pl.BlockSpec((tr, nb), lambda r: (r, 0)),
                   pl.BlockSpec((tr, nv), lambda r: (r, 0)),
                   pl.BlockSpec((tr, nf), lambda r: (r, 0))),
        compiler_params=pltpu.CompilerParams(
            dimension_semantics=("parallel",),
            vmem_limit_bytes=_vmem(need)),
        name="w_in_prep",
    )(w_in, tail)


def _split3(v):
    hi = v.astype(BF16)
    r1 = v - hi.astype(F32)
    mid = r1.astype(BF16)
    lo = (r1 - mid.astype(F32)).astype(BF16)
    return hi, mid, lo


def _gates_kernel(p_ref, bias_ref, ga_ref, gt_ref, carry_ref, *, hm, hf):
    r = pl.program_id(1)
    rows = p_ref.shape[0]

    @pl.when(r == 0)
    def _():
        carry_ref[...] = jnp.zeros_like(carry_ref)

    z = p_ref[...] + bias_ref[...]
    ls = jnp.minimum(z, 0.0) - jnp.log(1.0 + jnp.exp(-jnp.abs(z)))
    ri = lax.broadcasted_iota(jnp.int32, (rows, rows), 0)
    ci = lax.broadcasted_iota(jnp.int32, (rows, rows), 1)
    tri = jnp.where(ci <= ri, 1.0, 0.0).astype(BF16)
    hi, mid, lo = _split3(ls)
    cs = _dot(tri, hi) + _dot(tri, mid) + _dot(tri, lo)
    lane = lax.broadcasted_iota(jnp.int32, (rows, LANES), 1)
    run = cs + carry_ref[...]
    ga = jnp.where(lane < hm, z,
                   jnp.where(lane < 2 * hm, cs,
                             jnp.where(lane < 2 * hm + hf, run, 0.0)))
    ga_ref[...] = ga
    gt_ref[...] = ga.T
    carry_ref[...] = run[rows - 1:rows, :]


def _gates(p32, bias, col_block, bsz, seq, hm, hf, blk):
    return pl.pallas_call(
        functools.partial(_gates_kernel, hm=hm, hf=hf),
        out_shape=(jax.ShapeDtypeStruct((bsz, seq, LANES), F32),
                   jax.ShapeDtypeStruct((bsz, LANES, seq), F32)),
        grid=(bsz, seq // blk),
        in_specs=[
            pl.BlockSpec((None, blk, LANES), lambda b, r: (b, r, col_block)),
            pl.BlockSpec((1, LANES), lambda b, r: (0, 0)),
        ],
        out_specs=(pl.BlockSpec((None, blk, LANES), lambda b, r: (b, r, 0)),
                   pl.BlockSpec((None, LANES, blk), lambda b, r: (b, 0, r))),
        scratch_shapes=[pltpu.VMEM((1, LANES), F32)],
        compiler_params=pltpu.CompilerParams(
            dimension_semantics=("parallel", "arbitrary")),
        name="gates",
    )(p32, bias)


LOG2E = math.log2(math.e)


def _softmax_cols_step(s, vt, m, l, acc, shift=None):
    top = jnp.max(s, axis=0, keepdims=True)
    m_new = jnp.maximum(m, top if shift is None else top + shift)
    alpha = jnp.exp2(m - m_new)
    p = jnp.exp2(s - (m_new if shift is None else m_new - shift))
    l = alpha * l + jnp.sum(p, axis=0, keepdims=True)
    acc = alpha * acc + _dot(vt, p.astype(BF16))
    return m_new, l, acc


def _flash_cols(chains, mask_fn, n_full, n):
    def body(j, carries):
        return tuple(_softmax_cols_step(score(j), vt(j), *c, shift=shift)
                     for (score, vt, shift), c in zip(chains, carries))

    init = (jnp.full((1, n), NEG_BIG, F32), jnp.zeros((1, n), F32), jnp.zeros((HEAD_DIM, n), F32))
    carries = lax.fori_loop(0, n_full, body, (init,) * len(chains))
    outs = []
    for (score, vt, shift), c in zip(chains, carries):
        m, l, acc = _softmax_cols_step(mask_fn(score(n_full)), vt(n_full), *c, shift=shift)
        outs.append(acc / l)
    return outs


def _heads_per_step(heads, *block_offsets):
    return 2 if heads % 2 == 0 and all(o % 2 == 0 for o in block_offsets) else 1


def _diff_kernel(q_ref, k_ref, vt_ref, lam_ref, sub_ref, o_ref, *, tq, tk, chunk, lam_init, hp):
    i = pl.program_id(2)
    lane = lax.broadcasted_iota(jnp.int32, (tq, HEAD_DIM), 1)

    def keys(j):
        return pl.ds(pl.multiple_of(j * tk, tk), tk)

    def head(c):
        cols = slice(c * HEAD_DIM, (c + 1) * HEAD_DIM)
        qf = q_ref[:, cols].astype(F32) * (DIFF_QK_DIM ** -0.5 * LOG2E)
        q0 = jnp.where(lane < DIFF_QK_DIM, qf, 0.0)
        q1 = jnp.where(lane >= DIFF_QK_DIM, qf, 0.0)
        qq = jnp.concatenate([q0, q1], axis=0).astype(BF16)
        return (lambda j: _dot_nt(k_ref[keys(j), cols], qq), lambda j: vt_ref[cols, keys(j)], None)

    n_full = (i * tq) // tk

    def mask(s):
        row = lax.broadcasted_iota(jnp.int32, (tk, 2 * tq), 0)
        col = lax.broadcasted_iota(jnp.int32, (tk, 2 * tq), 1)
        qpos = i * tq + jnp.where(col >= tq, col - tq, col)
        kpos = n_full * tk + row
        return jnp.where((kpos // chunk) <= (qpos // chunk), s, NEG_BIG)

    outs = _flash_cols([head(c) for c in range(hp)], mask, n_full, 2 * tq)
    lp = lam_ref[...]
    lam = (jnp.exp(jnp.sum(lp[0:1, :] * lp[1:2, :], axis=1, keepdims=True))
           - jnp.exp(jnp.sum(lp[2:3, :] * lp[3:4, :], axis=1, keepdims=True)) + lam_init)
    for c, ot in enumerate(outs):
        dt = ot[:, :tq] - lam * ot[:, tq:]
        dt = dt * lax.rsqrt(jnp.mean(dt * dt, axis=0, keepdims=True) + NORM_EPS)
        o_ref[:, c * HEAD_DIM:(c + 1) * HEAD_DIM] = (
            dt.T * sub_ref[...] * (1.0 - lam_init)).astype(o_ref.dtype)


def _diff_attention(pb, vt, lam_p, subln, *, bsz, seq, heads, qcol, kcol, vrow, chunk, lam_init, tq, tk):
    hp = _heads_per_step(heads, qcol, kcol, vrow)
    hw = hp * HEAD_DIM
    need = 2 * (2 * seq * hw * 2) + 6 * hp * (2 * tq) * tk * 4
    return pl.pallas_call(
        functools.partial(_diff_kernel, tq=tq, tk=tk, chunk=chunk, lam_init=lam_init, hp=hp),
        out_shape=jax.ShapeDtypeStruct((bsz, seq, heads * HEAD_DIM), BF16),
        grid=(bsz, heads // hp, seq // tq),
        in_specs=[
            pl.BlockSpec((None, tq, hw), lambda b, h, i: (b, i, qcol // hp + h)),
            pl.BlockSpec((None, seq, hw), lambda b, h, i: (b, 0, kcol // hp + h)),
            pl.BlockSpec((hw, seq), lambda b, h, i: (vrow // hp + h, b)),
            pl.BlockSpec((4, DIFF_QK_DIM), lambda b, h, i: (0, 0)),
            pl.BlockSpec((1, HEAD_DIM), lambda b, h, i: (0, 0)),
        ],
        out_specs=pl.BlockSpec((None, tq, hw), lambda b, h, i: (b, i, h)),
        compiler_params=pltpu.CompilerParams(
            dimension_semantics=("parallel", "parallel", "arbitrary"),
            vmem_limit_bytes=_vmem(need)),
        name="diff_attention",
    )(pb, pb, vt, lam_p, subln)


def _fox_kernel(q_ref, k_ref, vt_ref, ga_ref, gt_ref, o_ref, ckb_ref, *, tq, tk, lane0, blk, hp):
    h = pl.program_id(1)
    i = pl.program_id(2)
    seq = k_ref.shape[0]

    @pl.when(i == 0)
    def _():
        for c in range(hp):
            pick = lax.broadcasted_iota(jnp.int32, (LANES, LANES), 0) == lane0 + h * hp + c
            sel = jnp.where(pick, 1.0, 0.0).astype(BF16)

            def body(r, carry, c=c, sel=sel):
                rows = pl.ds(pl.multiple_of(r * blk, blk), blk)
                hi, mid, lo = _split3(ga_ref[rows, :])
                ckb_ref[c, rows, :] = (_dot(hi, sel) + _dot(mid, sel) + _dot(lo, sel)) * LOG2E
                return carry
            lax.fori_loop(0, seq // blk, body, 0)

    def keys(j):
        return pl.ds(pl.multiple_of(j * tk, tk), tk)

    def head(c):
        cols = slice(c * HEAD_DIM, (c + 1) * HEAD_DIM)
        q = (q_ref[:, cols].astype(F32) * (HEAD_DIM ** -0.5 * LOG2E)).astype(BF16)
        cq = gt_ref[pl.ds(lane0 + h * hp + c, 1), :] * LOG2E

        def scores(j):
            ck = ckb_ref[c, keys(j), :]
            return _dot_nt(k_ref[keys(j), cols], q) - jnp.concatenate([ck] * (tq // LANES), axis=1)

        return (scores, lambda j: vt_ref[cols, keys(j)], cq)

    n_full = (i * tq) // tk

    def mask(s):
        row = lax.broadcasted_iota(jnp.int32, (tk, tq), 0)
        col = lax.broadcasted_iota(jnp.int32, (tk, tq), 1)
        return jnp.where(n_full * tk + row <= i * tq + col, s, NEG_BIG)

    outs = _flash_cols([head(c) for c in range(hp)], mask, n_full, tq)
    for c, ot in enumerate(outs):
        o_ref[:, c * HEAD_DIM:(c + 1) * HEAD_DIM] = ot.T.astype(o_ref.dtype)


def _fox_attention(pb, vt, ga, gt, *, bsz, seq, heads, qcol, kcol, vrow, lane0, tq, tk):
    hp = _heads_per_step(heads, qcol, kcol, vrow)
    hw = hp * HEAD_DIM
    need = 2 * (2 * seq * hw * 2) + (2 + hp) * LANES * seq * 4 + 6 * hp * tq * tk * 4
    return pl.pallas_call(
        functools.partial(_fox_kernel, tq=tq, tk=tk, lane0=lane0, blk=min(256, seq), hp=hp),
        out_shape=jax.ShapeDtypeStruct((bsz, seq, heads * HEAD_DIM), BF16),
        grid=(bsz, heads // hp, seq // tq),
        in_specs=[
            pl.BlockSpec((None, tq, hw), lambda b, h, i: (b, i, qcol // hp + h)),
            pl.BlockSpec((None, seq, hw), lambda b, h, i: (b, 0, kcol // hp + h)),
            pl.BlockSpec((hw, seq), lambda b, h, i: (vrow // hp + h, b)),
            pl.BlockSpec((None, seq, LANES), lambda b, h, i: (b, 0, 0)),
            pl.BlockSpec((None, LANES, tq), lambda b, h, i: (b, 0, i)),
        ],
        out_specs=pl.BlockSpec((None, tq, hw), lambda b, h, i: (b, i, h)),
        scratch_shapes=[pltpu.VMEM((hp, seq, LANES), F32)],
        compiler_params=pltpu.CompilerParams(
            dimension_semantics=("arbitrary", "arbitrary", "arbitrary"),
            vmem_limit_bytes=_vmem(need)),
        name="fox_attention",
    )(pb, pb, vt, ga, gt)


def _mlstm_kernel(q_ref, k_ref, o_gate_ref, v_ref, ga_ref, gt_ref, cwq_ref, cwk_ref, cbq_ref, cbk_ref,
                  g_ref, y_ref, qbuf, kbuf, c_ref, n_ref, m_ref, *, hm):
    h = pl.program_id(1)
    r = pl.program_id(2)
    ln = q_ref.shape[0]
    halo = SUBLANES

    @pl.when(r == 0)
    def _():
        qbuf[0:halo, :] = jnp.zeros((halo, HEAD_DIM), F32)
        kbuf[0:halo, :] = jnp.zeros((halo, HEAD_DIM), F32)
        c_ref[...] = jnp.zeros_like(c_ref)
        n_ref[...] = jnp.zeros_like(n_ref)
        m_ref[...] = jnp.zeros_like(m_ref)

    @pl.when(r > 0)
    def _():
        qbuf[0:halo, :] = qbuf[ln:ln + halo, :]
        kbuf[0:halo, :] = kbuf[ln:ln + halo, :]

    qbuf[halo:halo + ln, :] = q_ref[...]
    kbuf[halo:halo + ln, :] = k_ref[...]

    def conv_silu(buf, w_ref, b_ref):
        acc = b_ref[...] + w_ref[CONV_WIDTH - 1:CONV_WIDTH, :] * buf[halo:halo + ln, :]
        for t in range(1, CONV_WIDTH):
            acc = acc + w_ref[CONV_WIDTH - 1 - t:CONV_WIDTH - t, :] * buf[halo - t:halo - t + ln, :]
        return _silu(acc)

    q = conv_silu(qbuf, cwq_ref, cbq_ref)
    k = conv_silu(kbuf, cwk_ref, cbk_ref) * (HEAD_DIM ** -0.5)
    v = v_ref[...]

    lane = lax.broadcasted_iota(jnp.int32, (ln, LANES), 1)
    ga = ga_ref[...]
    ig_c = jnp.sum(jnp.where(lane == h, ga, 0.0), axis=1, keepdims=True)
    b_c = jnp.sum(jnp.where(lane == hm + h, ga, 0.0), axis=1, keepdims=True)
    ig_r = gt_ref[pl.ds(h, 1), :]
    b_r = gt_ref[pl.ds(hm + h, 1), :]
    g = b_r[:, ln - 1:ln]

    c_prev = c_ref[...]
    n_prev = n_ref[...]
    m_prev = m_ref[...]

    a = g - b_c + ig_c
    m_loc = jnp.max(a, axis=0, keepdims=True)
    kw = (k * jnp.exp(a - m_loc))
    c_loc = _dot_tn(kw.astype(BF16), v)
    n_loc = jnp.sum(kw, axis=0, keepdims=True)
    m_new = jnp.maximum(g + m_prev, m_loc)
    decay = jnp.exp(g + m_prev - m_new)
    scl = jnp.exp(m_loc - m_new)
    c_ref[...] = decay * c_prev + scl * c_loc
    n_ref[...] = decay * n_prev + scl * n_loc
    m_ref[...] = m_new

    row = lax.broadcasted_iota(jnp.int32, (ln, ln), 0)
    col = lax.broadcasted_iota(jnp.int32, (ln, ln), 1)
    log_d = jnp.where(col <= row, b_c - b_r + ig_r, NEG_BIG)
    log_inter = b_c + m_prev
    m_t = jnp.maximum(log_inter, jnp.max(log_d, axis=1, keepdims=True))
    qb = q.astype(BF16)
    s_qk = _dot_nt(qb, k.astype(BF16)) * jnp.exp(log_d - m_t)
    w_inter = jnp.exp(log_inter - m_t)
    num = w_inter * _dot(qb, c_prev.astype(BF16)) + _dot(s_qk.astype(BF16), v)
    den = w_inter * jnp.sum(q * n_prev, axis=1, keepdims=True) + jnp.sum(s_qk, axis=1, keepdims=True)
    hh = num / jnp.maximum(jnp.abs(den), jnp.exp(-m_t))
    hmix = _sigmoid(o_gate_ref[...]) * hh
    y_ref[...] = _rms_rows(hmix, g_ref[...]).astype(y_ref.dtype)


def _mlstm(p32, pb, ga, gt, conv_w, conv_b, gain, *, bsz, seq, hm, qcol, kcol, ocol, vcol, ln):
    width = hm * HEAD_DIM
    blk = lambda c: pl.BlockSpec((None, ln, HEAD_DIM), lambda b, h, r: (b, r, c + h))
    return pl.pallas_call(
        functools.partial(_mlstm_kernel, hm=hm),
        out_shape=jax.ShapeDtypeStruct((bsz, seq, width), BF16),
        grid=(bsz, hm, seq // ln),
        in_specs=[
            blk(qcol), blk(kcol), blk(ocol), blk(vcol),
            pl.BlockSpec((None, ln, LANES), lambda b, h, r: (b, r, 0)),
            pl.BlockSpec((None, LANES, ln), lambda b, h, r: (b, 0, r)),
            pl.BlockSpec((CONV_WIDTH, HEAD_DIM), lambda b, h, r: (0, h)),
            pl.BlockSpec((CONV_WIDTH, HEAD_DIM), lambda b, h, r: (0, hm + h)),
            pl.BlockSpec((1, HEAD_DIM), lambda b, h, r: (0, h)),
            pl.BlockSpec((1, HEAD_DIM), lambda b, h, r: (0, hm + h)),
            pl.BlockSpec((1, HEAD_DIM), lambda b, h, r: (0, h)),
        ],
        out_specs=pl.BlockSpec((None, ln, HEAD_DIM), lambda b, h, r: (b, r, h)),
        scratch_shapes=[
            pltpu.VMEM((ln + 2 * SUBLANES, HEAD_DIM), F32),
            pltpu.VMEM((ln + 2 * SUBLANES, HEAD_DIM), F32),
            pltpu.VMEM((HEAD_DIM, HEAD_DIM), F32),
            pltpu.VMEM((1, HEAD_DIM), F32),
            pltpu.VMEM((1, 1), F32),
        ],
        compiler_params=pltpu.CompilerParams(
            dimension_semantics=("parallel", "parallel", "arbitrary")),
        name="mlstm",
    )(p32, p32, p32, pb, ga, gt, conv_w, conv_w, conv_b, conv_b, gain)


def _oproj_kernel(x_ref, yd_ref, ym_ref, yf_ref, wd_ref, wm_ref, wf_ref, o_ref):
    acc = _dot(yd_ref[...], wd_ref[...])
    acc += _dot(ym_ref[...], wm_ref[...])
    acc += _dot(yf_ref[...], wf_ref[...])
    o_ref[...] = x_ref[...] + acc


def _oproj(x, yd, ym, yf, wd, wm, wf):
    m, d = x.shape
    tm = min(1024, m)
    tn = _pick_tn(d, 512)
    kd, km, kf = yd.shape[1], ym.shape[1], yf.shape[1]
    kk = kd + km + kf
    need = 2 * tm * kk * 2 + 2 * kk * tn * 2 + 5 * tm * tn * 4
    return pl.pallas_call(
        _oproj_kernel,
        out_shape=jax.ShapeDtypeStruct((m, d), F32),
        grid=(m // tm, d // tn),
        in_specs=[
            pl.BlockSpec((tm, tn), lambda i, j: (i, j)),
            pl.BlockSpec((tm, kd), lambda i, j: (i, 0)),
            pl.BlockSpec((tm, km), lambda i, j: (i, 0)),
            pl.BlockSpec((tm, kf), lambda i, j: (i, 0)),
            pl.BlockSpec((kd, tn), lambda i, j: (0, j)),
            pl.BlockSpec((km, tn), lambda i, j: (0, j)),
            pl.BlockSpec((kf, tn), lambda i, j: (0, j)),
        ],
        out_specs=pl.BlockSpec((tm, tn), lambda i, j: (i, j)),
        compiler_params=pltpu.CompilerParams(
            dimension_semantics=("parallel", "arbitrary"),
            vmem_limit_bytes=_vmem(need)),
        name="out_proj_residual",
    )(x, yd, ym, yf, wd, wm, wf)


def kernel(x, ffn1_norm, ffn1_w_gate, ffn1_w_up, ffn1_w_down, mix_norm, w_in, diff_lambda_q1, diff_lambda_k1, diff_lambda_q2, diff_lambda_k2, diff_subln, mlstm_conv_w, mlstm_conv_b, mlstm_b_i, mlstm_b_f, mlstm_norm, fox_b_f, w_out, ffn2_norm, ffn2_w_gate, ffn2_w_up, ffn2_w_down, final_norm):
    bsz, seq, d = x.shape
    depth = ffn1_norm.shape[0]
    m = bsz * seq
    n_heads = d // HEAD_DIM
    hm = n_heads // 4
    hd = (n_heads - hm) // 2
    hf = n_heads - hm - hd
    dw, mw, fw = hd * HEAD_DIM, hm * HEAD_DIM, hf * HEAD_DIM
    assert 2 * hm + hf <= LANES
    chunk = 64
    tq = min(256, seq)
    assert seq % tq == 0 and tq % chunk == 0

    o_mi = 3 * dw + 4 * mw
    n_in = o_mi + 2 * hm + 3 * fw + hf
    assert w_in.shape[2] == n_in

    xf = x.reshape(m, d)
    row = lambda v: v.reshape(1, -1).astype(F32)
    final_g = row(final_norm)

    for l in range(depth):
        lam_init = 0.8 - 0.6 * math.exp(-0.3 * l)
        last = l == depth - 1

        xf = _ffn(xf, row(ffn1_norm[l]), _cast_bf16(ffn1_w_gate, l), _cast_bf16(ffn1_w_up, l),
                  _cast_bf16(ffn1_w_down, l), final_g, final=False)

        tail_w = n_in - (o_mi + 3 * fw)
        tail = jnp.pad(w_in[l, :, o_mi + 3 * fw:], ((0, 0), (0, LANES - tail_w)))
        w_b, w_v, w_f = _win_prep(w_in, tail, l, dw=dw, mw=mw, fw=fw, sh=2 * hm, hf=hf)
        g_mix = row(mix_norm[l])
        pb = _proj(xf, g_mix, w_b, BF16, "mix_proj_bf16").reshape(bsz, seq, -1)
        vt = _proj(xf, g_mix, w_v, BF16, "mix_proj_vt", transpose_out=True)
        p32 = _proj(xf, g_mix, w_f, F32, "mix_proj_f32").reshape(bsz, seq, -1)

        gate_bias = jnp.concatenate([mlstm_b_i[l], mlstm_b_f[l], fox_b_f[l],
                                     jnp.zeros((LANES - 2 * hm - hf,), F32)]).reshape(1, LANES).astype(F32)
        ga, gt = _gates(p32, gate_bias, 3 * hm, bsz, seq, hm, hf, tq)

        lam_p = jnp.stack([diff_lambda_q1[l], diff_lambda_k1[l], diff_lambda_q2[l], diff_lambda_k2[l]]).astype(F32)
        y_diff = _diff_attention(pb, vt, lam_p, row(diff_subln[l]), bsz=bsz, seq=seq, heads=hd,
                                 qcol=0, kcol=hd, vrow=0, chunk=chunk, lam_init=lam_init,
                                 tq=tq, tk=min(512, seq))
        y_fox = _fox_attention(pb, vt, ga, gt, bsz=bsz, seq=seq, heads=hf,
                               qcol=2 * hd, kcol=2 * hd + hf, vrow=hd, lane0=2 * hm,
                               tq=min(512, seq), tk=min(512, seq))
        y_ml = _mlstm(p32, pb, ga, gt, mlstm_conv_w[l].astype(F32), row(mlstm_conv_b[l]), row(mlstm_norm[l]),
                      bsz=bsz, seq=seq, hm=hm, qcol=0, kcol=hm, ocol=2 * hm, vcol=2 * hd + 2 * hf, ln=tq)

        xf = _oproj(xf, y_diff.reshape(m, dw), y_ml.reshape(m, mw), y_fox.reshape(m, fw),
                    _cast_bf16(w_out, l, 0, dw), _cast_bf16(w_out, l, dw, mw),
                    _cast_bf16(w_out, l, dw + mw, fw))

        xf = _ffn(xf, row(ffn2_norm[l]), _cast_bf16(ffn2_w_gate, l), _cast_bf16(ffn2_w_up, l),
                  _cast_bf16(ffn2_w_down, l), final_g, final=last)

    return xf.reshape(bsz, seq, d)
```
